```python
import jax, jax.numpy as jnp
from jax import lax
import numpy as np

D_MODEL = 1024
BATCH = 4
SEQ = 4096
DEPTH = 4
DEC_BATCH = 128
DEC_SEQ = 8
PAST_LEN = 8192
PAGE_SIZE = 128

D_CONV = D_MODEL // 2
CONV_WIDTH = 31
D_GMLP = D_MODEL // 2
GMLP_GROUPS = 8
GMLP_GROUP_DIM = D_GMLP // GMLP_GROUPS
GMLP_CHUNK = 128
N_HEADS = 8
N_KV_HEADS = 2
HEAD_DIM = 64
Q_PER_KV = N_HEADS // N_KV_HEADS
WINDOW = 128
ATTN_BLOCK = WINDOW
ROPE_THETA = 10000.0
D_FF = ((8 * D_MODEL // 3 + 127) // 128) * 128
N_BRANCH = 3
EPS = 1e-6
NEG_INF = -1e30
C_GLU = 2 * D_CONV
C_GMLP = 2 * D_GMLP
C_Q = N_HEADS * HEAD_DIM
C_KV = N_KV_HEADS * HEAD_DIM
C_GATE = N_BRANCH * D_MODEL
D_IN = C_GLU + C_GMLP + C_Q + 2 * C_KV + C_GATE
SPLITS = (C_GLU, C_GLU + C_GMLP, C_GLU + C_GMLP + C_Q, C_GLU + C_GMLP + C_Q + C_KV, C_GLU + C_GMLP + C_Q + 2 * C_KV)

kernel_name = 'hybrid_conv_gmlp_swa_decoder_step'


def rms_norm(x, g):
    xf = x.astype(jnp.float32)
    y = xf * lax.rsqrt(jnp.mean(xf * xf, axis=-1, keepdims=True) + EPS)
    return y.astype(x.dtype) * g


def layer_norm(x, g, b):
    xf = x.astype(jnp.float32)
    mu = jnp.mean(xf, axis=-1, keepdims=True)
    xc = xf - mu
    y = xc * lax.rsqrt(jnp.mean(xc * xc, axis=-1, keepdims=True) + EPS)
    return y.astype(x.dtype) * g + b


def swiglu(x, w_gu, w_down):
    gu = x @ w_gu
    return (jax.nn.silu(gu[..., :D_FF]) * gu[..., D_FF:]) @ w_down


def rope(x, pos):
    half = HEAD_DIM // 2
    inv_freq = 1.0 / (ROPE_THETA ** (jnp.arange(half, dtype=jnp.float32) / half))
    ang = pos.astype(jnp.float32)[:, None] * inv_freq[None, :]
    cos = jnp.cos(ang)[None, :, None, :]
    sin = jnp.sin(ang)[None, :, None, :]
    xf = x.astype(jnp.float32)
    x1, x2 = xf[..., :half], xf[..., half:]
    return jnp.concatenate([x1 * cos - x2 * sin, x2 * cos + x1 * sin], axis=-1).astype(x.dtype)


def sink_attention(q, k, v, mask, sinks):
    s = jnp.einsum('...qkgd,...skd->...kgqs', q, k).astype(jnp.float32) * (HEAD_DIM ** -0.5)
    s = jnp.where(mask, s, NEG_INF)
    sk = sinks.astype(jnp.float32).reshape(N_KV_HEADS, Q_PER_KV, 1)
    m = jnp.maximum(jnp.max(s, axis=-1), sk)
    p = jnp.exp(s - m[..., None])
    denom = jnp.sum(p, axis=-1) + jnp.exp(sk - m)
    p = (p / denom[..., None]).astype(v.dtype)
    return jnp.einsum('...kgqs,...skd->...qkgd', p, v)


def attn_prompt(q, k, v, sinks):
    B, L = q.shape[0], q.shape[1]
    nb = L // ATTN_BLOCK
    qb = q.reshape(B, nb, ATTN_BLOCK, N_KV_HEADS, Q_PER_KV, HEAD_DIM)
    kb = k.reshape(B, nb, ATTN_BLOCK, N_KV_HEADS, HEAD_DIM)
    vb = v.reshape(B, nb, ATTN_BLOCK, N_KV_HEADS, HEAD_DIM)
    pad = ((0, 0), (1, 0), (0, 0), (0, 0), (0, 0))
    kk = jnp.concatenate([jnp.pad(kb[:, :-1], pad), kb], axis=2)
    vv = jnp.concatenate([jnp.pad(vb[:, :-1], pad), vb], axis=2)
    qi = jnp.arange(ATTN_BLOCK)[:, None]
    si = jnp.arange(2 * ATTN_BLOCK)[None, :]
    delta = qi + ATTN_BLOCK - si
    band = (delta >= 0) & (delta <= WINDOW)
    blk = jnp.arange(nb)[:, None, None]
    valid = band[None] & ((si[None] >= ATTN_BLOCK) | (blk >= 1))
    o = sink_attention(qb, kk, vv, valid[None, :, None, None], sinks)
    return o.reshape(B, L, N_HEADS * HEAD_DIM)


def attn_sample(q, k_new, v_new, k_past, v_past, sinks):
    B, T = q.shape[0], q.shape[1]
    lb = k_past.shape[1]
    kk = jnp.concatenate([k_past.astype(k_new.dtype), k_new], axis=1)
    vv = jnp.concatenate([v_past.astype(v_new.dtype), v_new], axis=1)
    delta = jnp.arange(T)[:, None] + lb - jnp.arange(lb + T)[None, :]
    mask = (delta >= 0) & (delta <= WINDOW)
    o = sink_attention(q.reshape(B, T, N_KV_HEADS, Q_PER_KV, HEAD_DIM), kk, vv, mask, sinks)
    return o.reshape(B, T, N_HEADS * HEAD_DIM)


def token_mixer(h, pos, conv_prev, k_past, v_past, is_prompt, w_in, w_conv_dw, b_conv_dw, ln_conv_g, ln_conv_b,
                w_conv_out, ln_gmlp_g, ln_gmlp_b, w_spatial, b_spatial, w_gmlp_out, q_norm_g, k_norm_g, sinks, w_o, w_out):
    B, L = h.shape[0], h.shape[1]
    z = h @ w_in
    a_in, g_in, q, k, v, gates = jnp.split(z, SPLITS, axis=-1)
    a = a_in[..., :D_CONV] * jax.nn.sigmoid(a_in[..., D_CONV:])
    xp = jnp.concatenate([conv_prev.astype(a.dtype), a], axis=1)
    c = lax.conv_general_dilated(xp, w_conv_dw[:, None, :].astype(xp.dtype), (1,), 'VALID',
                                 dimension_numbers=('NWC', 'WIO', 'NWC'), feature_group_count=D_CONV) + b_conv_dw
    y_a = jax.nn.silu(layer_norm(c, ln_conv_g, ln_conv_b)) @ w_conv_out
    conv_state = xp[:, -(CONV_WIDTH - 1):]
    g_in = jax.nn.gelu(g_in)
    u = g_in[..., :D_GMLP]
    vg = layer_norm(g_in[..., D_GMLP:], ln_gmlp_g, ln_gmlp_b)
    n = min(L, GMLP_CHUNK)
    vc = vg.reshape(B, L // n, n, GMLP_GROUPS, GMLP_GROUP_DIM)
    causal = jnp.arange(n)[:, None] >= jnp.arange(n)[None, :]
    w_s = jnp.where(causal, w_spatial[:, :n, :n], 0.0)
    sg = jnp.einsum('gts,bcsgd->bctgd', w_s, vc) + b_spatial[:, :n].T[None, None, :, :, None]
    y_b = (u * sg.reshape(B, L, D_GMLP)) @ w_gmlp_out
    q = rope(rms_norm(q.reshape(B, L, N_HEADS, HEAD_DIM), q_norm_g), pos)
    k = rope(rms_norm(k.reshape(B, L, N_KV_HEADS, HEAD_DIM), k_norm_g), pos)
    v = v.reshape(B, L, N_KV_HEADS, HEAD_DIM)
    if is_prompt:
        o = attn_prompt(q, k, v, sinks)
        k_rows, v_rows = k[:, -WINDOW:], v[:, -WINDOW:]
    else:
        o = attn_sample(q, k, v, k_past, v_past, sinks)
        k_rows, v_rows = k, v
    y_c = o @ w_o
    gt = jax.nn.sigmoid(gates).reshape(B, L, N_BRANCH, D_MODEL)
    mix = gt[..., 0, :] * y_a + gt[..., 1, :] * y_b + gt[..., 2, :] * y_c
    return mix @ w_out, conv_state, k_rows, v_rows, vg


def decoder_layer(x, pos, conv_prev, k_past, v_past, is_prompt, ffn_w, mixer_w):
    n1, gu1, dn1, nm, n2, gu2, dn2 = ffn_w
    x = x + 0.5 * swiglu(rms_norm(x, n1), gu1, dn1)
    m, conv_state, k_rows, v_rows, v_gmlp = token_mixer(rms_norm(x, nm), pos, conv_prev, k_past, v_past, is_prompt, *mixer_w)
    x = x + m
    x = x + 0.5 * swiglu(rms_norm(x, n2), gu2, dn2)
    return x, conv_state, k_rows, v_rows, v_gmlp


def setup_inputs(seed: int = 0) -> dict:
    key = jax.random.key(seed)
    ks = jax.random.split(key, 28)

    def nrm(k, shape, scale):
        return jax.random.normal(k, shape, jnp.float32) * scale

    def gain(k, shape):
        return 1.0 + 0.02 * jax.random.normal(k, shape, jnp.float32)

    win_buf = min(WINDOW, PAST_LEN)
    L = DEPTH
    return {
        'x_prompt': nrm(ks[0], (BATCH, SEQ, D_MODEL), 1.0),
        'x_sample': nrm(ks[1], (DEC_BATCH, DEC_SEQ, D_MODEL), 1.0),
        'state_conv': nrm(ks[2], (L, DEC_BATCH, CONV_WIDTH - 1, D_CONV), 0.5),
        'cache_k': nrm(ks[3], (L, DEC_BATCH, win_buf, N_KV_HEADS, HEAD_DIM), 1.0),
        'cache_v': nrm(ks[4], (L, DEC_BATCH, win_buf, N_KV_HEADS, HEAD_DIM), 1.0),
        'norm_ffn1': gain(ks[5], (L, D_MODEL)),
        'w_ffn1_gu': nrm(ks[6], (L, D_MODEL, 2 * D_FF), D_MODEL ** -0.5),
        'w_ffn1_down': nrm(ks[7], (L, D_FF, D_MODEL), D_FF ** -0.5),
        'norm_mix': gain(ks[8], (L, D_MODEL)),
        'w_in': nrm(ks[9], (L, D_MODEL, D_IN), D_MODEL ** -0.5),
        'w_conv_dw': nrm(ks[10], (L, CONV_WIDTH, D_CONV), CONV_WIDTH ** -0.5),
        'b_conv_dw': nrm(ks[11], (L, D_CONV), 0.02),
        'ln_conv_g': gain(ks[12], (L, D_CONV)),
        'ln_conv_b': nrm(ks[13], (L, D_CONV), 0.02),
        'w_conv_out': nrm(ks[14], (L, D_CONV, D_MODEL), D_CONV ** -0.5),
        'ln_gmlp_g': gain(ks[15], (L, D_GMLP)),
        'ln_gmlp_b': nrm(ks[16], (L, D_GMLP), 0.02),
        'w_spatial': nrm(ks[17], (L, GMLP_GROUPS, GMLP_CHUNK, GMLP_CHUNK), GMLP_CHUNK ** -0.5),
        'b_spatial': gain(ks[18], (L, GMLP_GROUPS, GMLP_CHUNK)),
        'w_gmlp_out': nrm(ks[19], (L, D_GMLP, D_MODEL), D_GMLP ** -0.5),
        'q_norm_g': gain(ks[20], (L, HEAD_DIM)),
        'k_norm_g': gain(ks[21], (L, HEAD_DIM)),
        'attn_sinks': nrm(ks[22], (L, N_HEADS), 1.0),
        'w_o': nrm(ks[23], (L, C_Q, D_MODEL), C_Q ** -0.5),
        'w_out': nrm(ks[24], (L, D_MODEL, D_MODEL), D_MODEL ** -0.5),
        'norm_ffn2': gain(ks[25], (L, D_MODEL)),
        'w_ffn2_gu': nrm(ks[26], (L, D_MODEL, 2 * D_FF), D_MODEL ** -0.5),
        'w_ffn2_down': nrm(ks[27], (L, D_FF, D_MODEL), D_FF ** -0.5),
    }


def reference(x_prompt, x_sample, state_conv, cache_k, cache_v, norm_ffn1, w_ffn1_gu, w_ffn1_down, norm_mix, w_in,
              w_conv_dw, b_conv_dw, ln_conv_g, ln_conv_b, w_conv_out, ln_gmlp_g, ln_gmlp_b, w_spatial, b_spatial,
              w_gmlp_out, q_norm_g, k_norm_g, attn_sinks, w_o, w_out, norm_ffn2, w_ffn2_gu, w_ffn2_down):
    bp, lp_len = x_prompt.shape[0], x_prompt.shape[1]
    pos_p = jnp.arange(lp_len, dtype=jnp.int32)
    pos_s = PAST_LEN + jnp.arange(x_sample.shape[1], dtype=jnp.int32)
    conv_zero = jnp.zeros((bp, CONV_WIDTH - 1, D_CONV), x_prompt.dtype)
    hp, hs = x_prompt, x_sample
    conv_p, conv_s, k_p, v_p, k_s, v_s, gv_s = [], [], [], [], [], [], []
    for l in range(DEPTH):
        ffn_w = (norm_ffn1[l], w_ffn1_gu[l], w_ffn1_down[l], norm_mix[l], norm_ffn2[l], w_ffn2_gu[l], w_ffn2_down[l])
        mixer_w = (w_in[l], w_conv_dw[l], b_conv_dw[l], ln_conv_g[l], ln_conv_b[l], w_conv_out[l], ln_gmlp_g[l],
                   ln_gmlp_b[l], w_spatial[l], b_spatial[l], w_gmlp_out[l], q_norm_g[l], k_norm_g[l], attn_sinks[l],
                   w_o[l], w_out[l])
        hp, cp, kp, vp, _ = decoder_layer(hp, pos_p, conv_zero, None, None, True, ffn_w, mixer_w)
        hs, cs, ks_, vs_, gv = decoder_layer(hs, pos_s, state_conv[l], cache_k[l], cache_v[l], False, ffn_w, mixer_w)
        conv_p.append(cp); conv_s.append(cs)
        k_p.append(kp); v_p.append(vp)
        k_s.append(ks_); v_s.append(vs_)
        gv_s.append(gv)
    return (hp, hs, jnp.stack(conv_p), jnp.stack(conv_s), jnp.stack(k_p), jnp.stack(v_p), jnp.stack(k_s), jnp.stack(v_s), jnp.stack(gv_s))
```

```python
import functools

import jax
import jax.numpy as jnp
from jax import lax
from jax.experimental import pallas as pl
from jax.experimental.pallas import tpu as pltpu

D_MODEL = 1024
DEPTH = 4
D_CONV = 512
CONV_WIDTH = 31
D_GMLP = 512
GMLP_GROUPS = 8
GMLP_GROUP_DIM = 64
GMLP_CHUNK = 128
N_HEADS = 8
N_KV_HEADS = 2
HEAD_DIM = 64
WINDOW = 128
ROPE_THETA = 10000.0
D_FF = 2816
EPS = 1e-6
NEG_INF = -1e30
PAST_LEN = 8192
C_GLU = 2 * D_CONV
C_GMLP = 2 * D_GMLP
C_Q = N_HEADS * HEAD_DIM
C_KV = N_KV_HEADS * HEAD_DIM
C_QKV = C_Q + 2 * C_KV
C_GATE = 3 * D_MODEL
D_IN = C_GLU + C_GMLP + C_QKV + C_GATE
OFF_GMLP = C_GLU
OFF_QKV = C_GLU + C_GMLP
OFF_GATE = OFF_QKV + C_QKV

LANES = 128
SUBLANES = 8
VMEM_LIMIT_BYTES = 56 * 1024 * 1024

FFN_ROWS = 512
FFN_CHUNK = 256
MIX_ROWS = 512
CONV_ROWS = 128
HALO = 32
SAMPLE_SEQS = 32
ATT_GROUP = 8

BF16 = jnp.bfloat16
F32 = jnp.float32


def _dot(a, b):
    return jnp.dot(a, b, preferred_element_type=F32)


def _dot_t(a, b):
    return lax.dot_general(a, b, (((1,), (1,)), ((), ())), preferred_element_type=F32)


def _rms(x, g):
    return x * lax.rsqrt(jnp.mean(x * x, axis=-1, keepdims=True) + EPS) * g


def _ln(x, g, b):
    mu = jnp.mean(x, axis=-1, keepdims=True)
    xc = x - mu
    return xc * lax.rsqrt(jnp.mean(xc * xc, axis=-1, keepdims=True) + EPS) * g + b


def _rope(x, cos, sin_signed):
    n = x.shape[1]
    lane = lax.broadcasted_iota(jnp.int32, x.shape, 1)
    first_half = (lane & (HEAD_DIM // 2)) == 0
    rot = jnp.where(first_half, pltpu.roll(x, n - HEAD_DIM // 2, 1), pltpu.roll(x, HEAD_DIM // 2, 1))
    return x * cos + rot * sin_signed


def _head_variants(x):
    lane = lax.broadcasted_iota(jnp.int32, x.shape, 1)
    low = lane < HEAD_DIM
    swapped = pltpu.roll(x, HEAD_DIM, 1)
    zero = jnp.zeros_like(x)
    h0 = (jnp.where(low, x, zero).astype(BF16), jnp.where(low, zero, swapped).astype(BF16))
    h1 = (jnp.where(low, swapped, zero).astype(BF16), jnp.where(low, zero, x).astype(BF16))
    return h0, h1


def _softmax_sink(s_parts, sink):
    m = sink
    for s in s_parts:
        m = jnp.maximum(jnp.max(s, axis=1, keepdims=True), m)
    p_parts = [jnp.exp(s - m) for s in s_parts]
    denom = jnp.exp(sink - m)
    for p in p_parts:
        denom = denom + jnp.sum(p, axis=1, keepdims=True)
    return p_parts, denom


def _ffn_kernel(x_ref, g_ref, wgu_ref, wd_ref, o_ref):
    x = x_ref[...]
    h = _rms(x, g_ref[...]).astype(BF16)
    acc = jnp.zeros((x.shape[0], D_MODEL), F32)
    for c in range(D_FF // FFN_CHUNK):
        lo = c * FFN_CHUNK
        gate = _dot(h, wgu_ref[:, lo:lo + FFN_CHUNK])
        up = _dot(h, wgu_ref[:, D_FF + lo:D_FF + lo + FFN_CHUNK])
        act = (jax.nn.silu(gate) * up).astype(BF16)
        acc = acc + _dot(act, wd_ref[lo:lo + FFN_CHUNK, :])
    o_ref[...] = x + 0.5 * acc


def _layer_spec(shape, layer):
    zeros = (0,) * len(shape)
    return pl.BlockSpec((None,) + tuple(shape), lambda *_: (layer,) + zeros, pipeline_mode=pl.Buffered(1))


def _ffn(x, layer, norm_g, w_gu, w_down):
    rows = x.shape[0]
    return pl.pallas_call(
        _ffn_kernel,
        grid=(rows // FFN_ROWS,),
        in_specs=[
            pl.BlockSpec((FFN_ROWS, D_MODEL), lambda i: (i, 0)),
            _layer_spec((1, D_MODEL), layer),
            _layer_spec((D_MODEL, 2 * D_FF), layer),
            _layer_spec((D_FF, D_MODEL), layer),
        ],
        out_specs=pl.BlockSpec((FFN_ROWS, D_MODEL), lambda i: (i, 0)),
        out_shape=jax.ShapeDtypeStruct(x.shape, F32),
        input_output_aliases={0: 0},
        compiler_params=pltpu.CompilerParams(
            dimension_semantics=("arbitrary",), vmem_limit_bytes=VMEM_LIMIT_BYTES),
        name="ffn",
    )(x, norm_g, w_gu, w_down)


def _glu_input(h, w_in_ref):
    a_in = _dot(h, w_in_ref[:, 0:C_GLU])
    return a_in[:, :D_CONV] * jax.nn.sigmoid(a_in[:, D_CONV:])


def _gmlp_input(h, w_in_ref, g_ref, b_ref):
    g_in = jax.nn.gelu(_dot(h, w_in_ref[:, OFF_GMLP:OFF_GMLP + C_GMLP]))
    return g_in[:, :D_GMLP], _ln(g_in[:, D_GMLP:], g_ref[...], b_ref[...])


def _qkv(h, w_in_ref, bdq_ref, bdk_ref, gq_ref, gk_ref, cos, sin_signed):
    qkv = _dot(h, w_in_ref[:, OFF_QKV:OFF_QKV + C_QKV])
    q = qkv[:, :C_Q]
    k = qkv[:, C_Q:C_Q + C_KV]
    v = qkv[:, C_Q + C_KV:]
    q_ms = _dot((q * q).astype(BF16), bdq_ref[...])
    k_ms = _dot((k * k).astype(BF16), bdk_ref[...])
    qn = q * lax.rsqrt(q_ms + EPS) * gq_ref[...]
    kn = k * lax.rsqrt(k_ms + EPS) * gk_ref[...]
    cos4 = jnp.concatenate([cos] * (C_Q // C_KV), axis=1)
    sin4 = jnp.concatenate([sin_signed] * (C_Q // C_KV), axis=1)
    qr = _rope(qn, cos4, sin4) * (HEAD_DIM ** -0.5)
    kr = _rope(kn, cos, sin_signed)
    return qr, kr, v


def _merge_out(x, h, y_a, y_b, y_c, w_in_ref, w_out_ref):
    gates = jax.nn.sigmoid(_dot(h, w_in_ref[:, OFF_GATE:OFF_GATE + C_GATE]))
    mix = (gates[:, :D_MODEL] * y_a + gates[:, D_MODEL:2 * D_MODEL] * y_b
           + gates[:, 2 * D_MODEL:] * y_c)
    return x + _dot(mix.astype(BF16), w_out_ref[...])


def _prompt_mixer_kernel(
        sinks_ref, x_ref, nm_ref, w_in_ref, wdw_ref, bdw_ref, lncg_ref, lncb_ref, wco_ref,
        lngg_ref, lngb_ref, wsp_ref, bsp_ref, wgo_ref, gq_ref, gk_ref, bdq_ref, bdk_ref,
        cos_ref, sin_ref, wo_ref, wout_ref,
        xo_ref, cst_ref, kst_ref, vst_ref,
        xp_scr, y_scr, act_scr, sg_scr, q_scr, k_scr, v_scr, o_scr, *, layer):
    s = pl.program_id(1)
    last_s = pl.num_programs(1) - 1
    rows = MIX_ROWS
    nq = rows // WINDOW

    @pl.when(s == 0)
    def _():
        xp_scr[0:HALO, :] = jnp.zeros((HALO, D_CONV), F32)
        xp_scr[HALO + rows:HALO + rows + SUBLANES, :] = jnp.zeros((SUBLANES, D_CONV), F32)
        k_scr[:, :, 0:WINDOW, :] = jnp.zeros((N_KV_HEADS, 2, WINDOW, C_KV), BF16)
        v_scr[:, :, 0:WINDOW, :] = jnp.zeros((N_KV_HEADS, 2, WINDOW, C_KV), BF16)

    x = x_ref[...]
    h = _rms(x, nm_ref[...]).astype(BF16)

    a = _glu_input(h, w_in_ref)
    xp_scr[HALO:HALO + rows, :] = a
    front = HALO - (CONV_WIDTH - 1)

    @pl.when(s == last_s)
    def _():
        cst_ref[...] = xp_scr[HALO + rows - (CONV_WIDTH - 1):HALO + rows, :]

    for c0 in range(0, rows, CONV_ROWS):
        for r in range(SUBLANES):
            part = None
            for q in range((HALO + SUBLANES) // SUBLANES):
                j = q * SUBLANES + r - front
                if 0 <= j < CONV_WIDTH:
                    term = wdw_ref[j:j + 1, :] * xp_scr[c0 + q * SUBLANES:c0 + q * SUBLANES + CONV_ROWS + SUBLANES, :]
                    part = term if part is None else part + term
            y_scr[r] = part
        conv = bdw_ref[...] + y_scr[0, 0:CONV_ROWS, :]
        for r in range(1, SUBLANES):
            conv = conv + y_scr[r, r:r + CONV_ROWS, :]
        act = jax.nn.silu(_ln(conv, lncg_ref[...], lncb_ref[...]))
        act_scr[c0:c0 + CONV_ROWS, :] = act.astype(BF16)
    xp_scr[front:HALO, :] = xp_scr[rows + front:rows + HALO, :]
    y_a = _dot(act_scr[...], wco_ref[...])

    u, vg = _gmlp_input(h, w_in_ref, lngg_ref, lngb_ref)
    lane = lax.broadcasted_iota(jnp.int32, (GMLP_CHUNK, LANES), 1)
    low = lane < GMLP_GROUP_DIM
    tri_row = lax.broadcasted_iota(jnp.int32, (GMLP_CHUNK, 2 * GMLP_CHUNK), 0)
    tri_col = lax.broadcasted_iota(jnp.int32, (GMLP_CHUNK, 2 * GMLP_CHUNK), 1) & (GMLP_CHUNK - 1)
    causal = tri_row >= tri_col
    n_chunks = rows // GMLP_CHUNK
    for p in range(GMLP_GROUPS // 2):
        w_pair = jnp.where(causal, wsp_ref[p], jnp.zeros_like(wsp_ref[p]))
        lo_parts, hi_parts = [], []
        for c in range(n_chunks):
            blk = vg[c * GMLP_CHUNK:(c + 1) * GMLP_CHUNK, p * LANES:(p + 1) * LANES]
            lo_parts.append(jnp.where(low, blk, 0.0))
            hi_parts.append(jnp.where(low, 0.0, blk))
        rhs = jnp.concatenate(
            [jnp.concatenate(lo_parts, axis=1), jnp.concatenate(hi_parts, axis=1)], axis=0).astype(BF16)
        mixed = _dot(w_pair, rhs)
        for c in range(n_chunks):
            sg_scr[c * GMLP_CHUNK:(c + 1) * GMLP_CHUNK, p * LANES:(p + 1) * LANES] = (
                mixed[:, c * LANES:(c + 1) * LANES] + bsp_ref[:, p * LANES:(p + 1) * LANES])
    y_b = _dot((u * sg_scr[...]).astype(BF16), wgo_ref[...])

    qr, kr, v = _qkv(h, w_in_ref, bdq_ref, bdk_ref, gq_ref, gk_ref, cos_ref[...], sin_ref[...])

    @pl.when(s == last_s)
    def _():
        kst_ref[...] = kr[rows - WINDOW:, :]
        vst_ref[...] = v[rows - WINDOW:, :]

    q_scr[...] = qr.astype(BF16)
    for kv, variants in enumerate(_head_variants(kr)):
        for half, val in enumerate(variants):
            k_scr[kv, half, WINDOW:WINDOW + rows, :] = val
    for kv, variants in enumerate(_head_variants(v)):
        for half, val in enumerate(variants):
            v_scr[kv, half, WINDOW:WINDOW + rows, :] = val

    qi = lax.broadcasted_iota(jnp.int32, (WINDOW, 2 * WINDOW), 0)
    si = lax.broadcasted_iota(jnp.int32, (WINDOW, 2 * WINDOW), 1)
    delta = qi + WINDOW - si
    band = (delta >= 0) & (delta <= WINDOW)
    bias_std = jnp.where(band, 0.0, NEG_INF)
    bias_first = jnp.where(band & (si >= WINDOW), 0.0, NEG_INF)
    low_q = lax.broadcasted_iota(jnp.int32, (WINDOW, LANES), 1) < HEAD_DIM
    for n in range(nq):
        bias = jnp.where(s == 0, bias_first, bias_std) if n == 0 else bias_std
        r0 = n * WINDOW
        for slab in range(C_Q // LANES):
            kv = slab // (C_Q // LANES // N_KV_HEADS)
            q_blk = q_scr[r0:r0 + WINDOW, slab * LANES:(slab + 1) * LANES]
            outs, inv = [], []
            for half in range(2):
                sink = sinks_ref[layer, 2 * slab + half]
                sc = _dot_t(q_blk, k_scr[kv, half, r0:r0 + 2 * WINDOW, :]) + bias
                (p,), denom = _softmax_sink([sc], sink)
                outs.append(_dot(p.astype(BF16), v_scr[kv, half, r0:r0 + 2 * WINDOW, :]))
                inv.append(1.0 / denom)
            o_blk = (outs[0] + outs[1]) * jnp.where(low_q, inv[0], inv[1])
            o_scr[r0:r0 + WINDOW, slab * LANES:(slab + 1) * LANES] = o_blk.astype(BF16)
    k_scr[:, :, 0:WINDOW, :] = k_scr[:, :, rows:rows + WINDOW, :]
    v_scr[:, :, 0:WINDOW, :] = v_scr[:, :, rows:rows + WINDOW, :]
    y_c = _dot(o_scr[...], wo_ref[...])

    xo_ref[...] = _merge_out(x, h, y_a, y_b, y_c, w_in_ref, wout_ref)


def _prompt_mixer(x, layer, batch, seq, sinks, params, cos, sin_signed):
    n_blocks = seq // MIX_ROWS
    param_specs = [_layer_spec(p.shape[1:], layer) for p in params]
    n_lead = 2 + 15
    in_specs = (
        [pl.BlockSpec(memory_space=pltpu.SMEM),
         pl.BlockSpec((MIX_ROWS, D_MODEL), lambda b, s: (b * n_blocks + s, 0))]
        + param_specs[:n_lead - 1]
        + [pl.BlockSpec((MIX_ROWS, C_KV), lambda b, s: (s, 0)),
           pl.BlockSpec((MIX_ROWS, C_KV), lambda b, s: (s, 0))]
        + param_specs[n_lead - 1:])
    out_shape = (
        jax.ShapeDtypeStruct(x.shape, F32),
        jax.ShapeDtypeStruct((batch, CONV_WIDTH - 1, D_CONV), F32),
        jax.ShapeDtypeStruct((batch, WINDOW, C_KV), F32),
        jax.ShapeDtypeStruct((batch, WINDOW, C_KV), F32),
    )
    out_specs = (
        pl.BlockSpec((MIX_ROWS, D_MODEL), lambda b, s: (b * n_blocks + s, 0)),
        pl.BlockSpec((None, CONV_WIDTH - 1, D_CONV), lambda b, s: (b, 0, 0)),
        pl.BlockSpec((None, WINDOW, C_KV), lambda b, s: (b, 0, 0)),
        pl.BlockSpec((None, WINDOW, C_KV), lambda b, s: (b, 0, 0)),
    )
    scratch = [
        pltpu.VMEM((HALO + MIX_ROWS + SUBLANES, D_CONV), F32),
        pltpu.VMEM((SUBLANES, CONV_ROWS + SUBLANES, D_CONV), F32),
        pltpu.VMEM((MIX_ROWS, D_CONV), BF16),
        pltpu.VMEM((MIX_ROWS, D_GMLP), F32),
        pltpu.VMEM((MIX_ROWS, C_Q), BF16),
        pltpu.VMEM((N_KV_HEADS, 2, WINDOW + MIX_ROWS, C_KV), BF16),
        pltpu.VMEM((N_KV_HEADS, 2, WINDOW + MIX_ROWS, C_KV), BF16),
        pltpu.VMEM((MIX_ROWS, C_Q), BF16),
    ]
    return pl.pallas_call(
        functools.partial(_prompt_mixer_kernel, layer=layer),
        grid=(batch, n_blocks),
        in_specs=in_specs,
        out_specs=out_specs,
        out_shape=out_shape,
        scratch_shapes=scratch,
        input_output_aliases={1: 0},
        compiler_params=pltpu.CompilerParams(
            dimension_semantics=("arbitrary", "arbitrary"), vmem_limit_bytes=VMEM_LIMIT_BYTES),
        name="prompt_mixer",
    )(sinks, x, *params[:n_lead - 1], cos, sin_signed, *params[n_lead - 1:])


def _sample_mixer_kernel(
        sinks_ref, x_ref, st_ref, ck_ref, cv_ref, nm_ref, w_in_ref, wdw_ref, bdw_ref, lncg_ref, lncb_ref,
        wco_ref, lngg_ref, lngb_ref, w8_ref, b8_ref, wgo_ref, gq_ref, gk_ref, bdq_ref, bdk_ref,
        cos_ref, sin_ref, wo_ref, wout_ref,
        xo_ref, cst_ref, kn_ref, vn_ref, gv_ref,
        c_scr, sg_scr, o_scr, *, layer, steps):
    gs = SAMPLE_SEQS
    rows = steps * gs
    x = x_ref[...].reshape(rows, D_MODEL)
    h = _rms(x, nm_ref[...]).astype(BF16)

    a = _glu_input(h, w_in_ref)
    hist = CONV_WIDTH - 1

    def conv_in(idx):
        if idx < hist:
            return st_ref[idx]
        return a[(idx - hist) * gs:(idx - hist + 1) * gs, :]

    for t in range(steps):
        acc = bdw_ref[...] + wdw_ref[0:1, :] * conv_in(t)
        for j in range(1, CONV_WIDTH):
            acc = acc + wdw_ref[j:j + 1, :] * conv_in(t + j)
        c_scr[t * gs:(t + 1) * gs, :] = acc
    for j in range(hist):
        cst_ref[j] = conv_in(steps + j)
    y_a = _dot(jax.nn.silu(_ln(c_scr[...], lncg_ref[...], lncb_ref[...])).astype(BF16), wco_ref[...])

    u, vg = _gmlp_input(h, w_in_ref, lngg_ref, lngb_ref)
    gv_ref[...] = vg.reshape(steps, gs, D_GMLP)
    for t in range(steps):
        acc = b8_ref[t:t + 1, :] + w8_ref[t, 0:1, :] * vg[0:gs, :]
        for s_ in range(1, t + 1):
            acc = acc + w8_ref[t, s_:s_ + 1, :] * vg[s_ * gs:(s_ + 1) * gs, :]
        sg_scr[t * gs:(t + 1) * gs, :] = acc
    y_b = _dot((u * sg_scr[...]).astype(BF16), wgo_ref[...])

    cos = cos_ref[...].reshape(rows, C_KV)
    sin_signed = sin_ref[...].reshape(rows, C_KV)
    qr, kr, v = _qkv(h, w_in_ref, bdq_ref, bdk_ref, gq_ref, gk_ref, cos, sin_signed)
    kn_ref[...] = kr.reshape(steps, gs, C_KV)
    vn_ref[...] = v.reshape(steps, gs, C_KV)

    g = ATT_GROUP
    nrow = steps * g
    ncache = g * WINDOW
    g_bits, w_bits = g.bit_length() - 1, WINDOW.bit_length() - 1
    row_c = lax.broadcasted_iota(jnp.int32, (nrow, ncache), 0)
    col_c = lax.broadcasted_iota(jnp.int32, (nrow, ncache), 1)
    q_t, q_j = row_c >> g_bits, row_c & (g - 1)
    c_j, c_s = col_c >> w_bits, col_c & (WINDOW - 1)
    bias_cache = jnp.where((q_j == c_j) & (c_s >= q_t), 0.0, NEG_INF)
    row_n = lax.broadcasted_iota(jnp.int32, (nrow, nrow), 0)
    col_n = lax.broadcasted_iota(jnp.int32, (nrow, nrow), 1)
    bias_new = jnp.where(((row_n & (g - 1)) == (col_n & (g - 1))) & ((col_n >> g_bits) <= (row_n >> g_bits)),
                         0.0, NEG_INF)
    low_q = lax.broadcasted_iota(jnp.int32, (nrow, LANES), 1) < HEAD_DIM

    def gather(arr, gi):
        return jnp.concatenate([arr[t * gs + gi * g:t * gs + (gi + 1) * g, :] for t in range(steps)], axis=0)

    for gi in range(gs // g):
        q_g = gather(qr, gi)
        kc = _head_variants(ck_ref[gi * ncache:(gi + 1) * ncache, :])
        vc = _head_variants(cv_ref[gi * ncache:(gi + 1) * ncache, :])
        kn = _head_variants(gather(kr, gi))
        vn = _head_variants(gather(v, gi))
        o_parts = []
        for slab in range(C_Q // LANES):
            kv = slab // (C_Q // LANES // N_KV_HEADS)
            q_blk = q_g[:, slab * LANES:(slab + 1) * LANES].astype(BF16)
            outs, inv = [], []
            for half in range(2):
                sink = sinks_ref[layer, 2 * slab + half]
                s_c = _dot_t(q_blk, kc[kv][half]) + bias_cache
                s_n = _dot_t(q_blk, kn[kv][half]) + bias_new
                (p_c, p_n), denom = _softmax_sink([s_c, s_n], sink)
                outs.append(_dot(p_c.astype(BF16), vc[kv][half]) + _dot(p_n.astype(BF16), vn[kv][half]))
                inv.append(1.0 / denom)
            o_parts.append((outs[0] + outs[1]) * jnp.where(low_q, inv[0], inv[1]))
        o_g = jnp.concatenate(o_parts, axis=1)
        for t in range(steps):
            o_scr[t * gs + gi * g:t * gs + (gi + 1) * g, :] = o_g[t * g:(t + 1) * g, :]
    y_c = _dot(o_scr[...].astype(BF16), wo_ref[...])

    xo_ref[...] = _merge_out(x, h, y_a, y_b, y_c, w_in_ref, wout_ref).reshape(steps, gs, D_MODEL)


def _sample_mixer(x, layer, sinks, state_t, cache_k, cache_v, params, cos, sin_signed):
    steps, n_seq = x.shape[0], x.shape[1]
    gs = SAMPLE_SEQS
    hist = CONV_WIDTH - 1
    param_specs = [_layer_spec(p.shape[1:], layer) for p in params]
    n_lead = 16
    tok3 = lambda width: pl.BlockSpec((steps, gs, width), lambda i: (0, i, 0))
    in_specs = (
        [pl.BlockSpec(memory_space=pltpu.SMEM),
         tok3(D_MODEL),
         pl.BlockSpec((None, hist, gs, D_CONV), lambda i: (layer, 0, i, 0)),
         pl.BlockSpec((None, gs * WINDOW, C_KV), lambda i: (layer, i, 0)),
         pl.BlockSpec((None, gs * WINDOW, C_KV), lambda i: (layer, i, 0))]
        + param_specs[:n_lead]
        + [tok3(C_KV), tok3(C_KV)]
        + param_specs[n_lead:])
    out_shape = (
        jax.ShapeDtypeStruct(x.shape, F32),
        jax.ShapeDtypeStruct((hist, n_seq, D_CONV), F32),
        jax.ShapeDtypeStruct((steps, n_seq, C_KV), F32),
        jax.ShapeDtypeStruct((steps, n_seq, C_KV), F32),
        jax.ShapeDtypeStruct((steps, n_seq, D_GMLP), F32),
    )
    out_specs = (
        tok3(D_MODEL),
        pl.BlockSpec((hist, gs, D_CONV), lambda i: (0, i, 0)),
        tok3(C_KV), tok3(C_KV), tok3(D_GMLP),
    )
    scratch = [
        pltpu.VMEM((steps * gs, D_CONV), F32),
        pltpu.VMEM((steps * gs, D_GMLP), F32),
        pltpu.VMEM((steps * gs, C_Q), F32),
    ]
    return pl.pallas_call(
        functools.partial(_sample_mixer_kernel, layer=layer, steps=steps),
        grid=(n_seq // gs,),
        in_specs=in_specs,
        out_specs=out_specs,
        out_shape=out_shape,
        scratch_shapes=scratch,
        input_output_aliases={1: 0},
        compiler_params=pltpu.CompilerParams(
            dimension_semantics=("arbitrary",), vmem_limit_bytes=VMEM_LIMIT_BYTES),
        name="sample_mixer",
    )(sinks, x, state_t, cache_k, cache_v, *params[:n_lead], cos, sin_signed, *params[n_lead:])


def _rope_tables(pos):
    half = HEAD_DIM // 2
    inv_freq = 1.0 / (ROPE_THETA ** (jnp.arange(half, dtype=F32) / half))
    ang = pos.astype(F32)[:, None] * inv_freq[None, :]
    cos, sin = jnp.cos(ang), jnp.sin(ang)
    cos_h = jnp.concatenate([cos, cos], axis=1)
    sin_h = jnp.concatenate([-sin, sin], axis=1)
    return jnp.tile(cos_h, (1, N_KV_HEADS)), jnp.tile(sin_h, (1, N_KV_HEADS))


def _block_diag_mean(width):
    head = jnp.arange(width) // HEAD_DIM
    return jnp.where(head[:, None] == head[None, :], 1.0 / HEAD_DIM, 0.0).astype(BF16)


def kernel(x_prompt, x_sample, state_conv, cache_k, cache_v, norm_ffn1, w_ffn1_gu, w_ffn1_down, norm_mix, w_in,
           w_conv_dw, b_conv_dw, ln_conv_g, ln_conv_b, w_conv_out, ln_gmlp_g, ln_gmlp_b, w_spatial, b_spatial,
           w_gmlp_out, q_norm_g, k_norm_g, attn_sinks, w_o, w_out, norm_ffn2, w_ffn2_gu, w_ffn2_down):
    batch, seq, _ = x_prompt.shape
    n_seq, steps, _ = x_sample.shape
    depth = w_in.shape[0]
    assert seq % MIX_ROWS == 0 and (batch * seq) % FFN_ROWS == 0 and (n_seq * steps) % FFN_ROWS == 0
    assert n_seq % SAMPLE_SEQS == 0 and SAMPLE_SEQS % ATT_GROUP == 0 and steps <= GMLP_CHUNK
    assert cache_k.shape[2] == WINDOW

    row = lambda p: p.reshape(depth, 1, -1)
    bf = lambda p: p.astype(BF16)
    head_tile = lambda p, n: jnp.tile(p, (1, n)).reshape(depth, 1, -1)
    common_front = [
        row(norm_mix), bf(w_in), w_conv_dw, row(b_conv_dw), row(ln_conv_g), row(ln_conv_b), bf(w_conv_out),
        row(ln_gmlp_g), row(ln_gmlp_b)]
    common_back = [
        bf(w_gmlp_out), head_tile(q_norm_g, N_HEADS), head_tile(k_norm_g, N_KV_HEADS),
        jnp.broadcast_to(_block_diag_mean(C_Q), (depth, C_Q, C_Q)),
        jnp.broadcast_to(_block_diag_mean(C_KV), (depth, C_KV, C_KV)),
        bf(w_o), bf(w_out)]
    wsp_pairs = bf(w_spatial.reshape(depth, GMLP_GROUPS // 2, 2, GMLP_CHUNK, GMLP_CHUNK)
                   .transpose(0, 1, 3, 2, 4).reshape(depth, GMLP_GROUPS // 2, GMLP_CHUNK, 2 * GMLP_CHUNK))
    bsp_full = jnp.repeat(b_spatial.transpose(0, 2, 1), GMLP_GROUP_DIM, axis=2)
    prompt_params = common_front + [wsp_pairs, bsp_full] + common_back
    w8 = jnp.repeat(w_spatial[:, :, :steps, :steps].transpose(0, 2, 3, 1), GMLP_GROUP_DIM, axis=3)
    b8 = jnp.repeat(b_spatial[:, :, :steps].transpose(0, 2, 1), GMLP_GROUP_DIM, axis=2)
    sample_params = common_front + [w8, b8] + common_back

    cos_p, sin_p = _rope_tables(jnp.arange(seq, dtype=jnp.int32))
    cos_s, sin_s = _rope_tables(PAST_LEN + jnp.arange(steps, dtype=jnp.int32))
    cos_s = jnp.broadcast_to(cos_s[:, None, :], (steps, n_seq, C_KV))
    sin_s = jnp.broadcast_to(sin_s[:, None, :], (steps, n_seq, C_KV))

    ffn1 = (row(norm_ffn1), bf(w_ffn1_gu), bf(w_ffn1_down))
    ffn2 = (row(norm_ffn2), bf(w_ffn2_gu), bf(w_ffn2_down))
    state_t = state_conv.transpose(0, 2, 1, 3)
    ck = cache_k.reshape(depth, n_seq * WINDOW, C_KV)
    cv = cache_v.reshape(depth, n_seq * WINDOW, C_KV)

    xp = x_prompt.reshape(batch * seq, D_MODEL)
    xs = x_sample.transpose(1, 0, 2)
    conv_p, conv_s, k_p, v_p, k_s, v_s, gv_s = [], [], [], [], [], [], []
    for l in range(depth):
        xp = _ffn(xp, l, *ffn1)
        xs = _ffn(xs.reshape(steps * n_seq, D_MODEL), l, *ffn1).reshape(steps, n_seq, D_MODEL)
        xp, cp, kp, vp = _prompt_mixer(xp, l, batch, seq, attn_sinks, prompt_params, cos_p, sin_p)
        xs, cs, ks, vs, gv = _sample_mixer(xs, l, attn_sinks, state_t, ck, cv, sample_params, cos_s, sin_s)
        xp = _ffn(xp, l, *ffn2)
        xs = _ffn(xs.reshape(steps * n_seq, D_MODEL), l, *ffn2).reshape(steps, n_seq, D_MODEL)
        conv_p.append(cp)
        conv_s.append(cs.transpose(1, 0, 2))
        k_p.append(kp.reshape(batch, WINDOW, N_KV_HEADS, HEAD_DIM))
        v_p.append(vp.reshape(batch, WINDOW, N_KV_HEADS, HEAD_DIM))
        k_s.append(ks.transpose(1, 0, 2).reshape(n_seq, steps, N_KV_HEADS, HEAD_DIM))
        v_s.append(vs.transpose(1, 0, 2).reshape(n_seq, steps, N_KV_HEADS, HEAD_DIM))
        gv_s.append(gv.transpose(1, 0, 2))
    return (xp.reshape(batch, seq, D_MODEL), xs.transpose(1, 0, 2), jnp.stack(conv_p), jnp.stack(conv_s),
            jnp.stack(k_p), jnp.stack(v_p), jnp.stack(k_s), jnp.stack(v_s), jnp.stack(gv_s))
```
